```python
import math
import jax, jax.numpy as jnp
from jax import lax
import numpy as np

D_MODEL = 2048
BATCH = 1
SEQ = 8192
DEPTH = 4

BRANCH_WIDTH = D_MODEL // 2
N_BRANCH = 4
CONV_WIDTH = 4
CHUNK = 64
GDN_DK = 128
GDN_DV = 128
GDN_HEADS = BRANCH_WIDTH // GDN_DV
SSD_HEAD_DIM = 64
SSD_HEADS = BRANCH_WIDTH // SSD_HEAD_DIM
SSD_GROUPS = 2
SSD_STATE = 128
RET_DK = 128
RET_DV = 128
RET_HEADS = BRANCH_WIDTH // RET_DV
GLA_DK = 128
GLA_DV = 256
GLA_HEADS = BRANCH_WIDTH // GLA_DV
GLA_GATE_RANK = 16
GLA_TAU = 16.0
GLA_CHUNK = 16
D_FF = 11 * D_MODEL // 4
N_EXPERTS = 8
TOP_K = 2
D_FF_EXPERT = D_FF
ROPE_BASE = 10000.0
EPS = 1e-6

GDN_QK = GDN_HEADS * GDN_DK
SSD_BC = SSD_GROUPS * SSD_STATE
RET_QK = RET_HEADS * RET_DK
GLA_QK = GLA_HEADS * GLA_DK
IN_SPLITS = (2 * GDN_QK + BRANCH_WIDTH, GDN_HEADS, GDN_HEADS, BRANCH_WIDTH,
             BRANCH_WIDTH, BRANCH_WIDTH + 2 * SSD_BC, SSD_HEADS,
             RET_QK, RET_QK, BRANCH_WIDTH, BRANCH_WIDTH,
             GLA_QK, GLA_QK, BRANCH_WIDTH, GLA_GATE_RANK, BRANCH_WIDTH,
             N_BRANCH * D_MODEL)
IN_TOTAL = sum(IN_SPLITS)

kernel_name = 'hybrid_gated_branch_trunk'


def rms_norm(x, w=None):
    xf = x.astype(jnp.float32)
    y = xf * lax.rsqrt(jnp.mean(xf * xf, axis=-1, keepdims=True) + EPS)
    if w is not None:
        y = y * w.astype(jnp.float32)
    return y.astype(x.dtype)


def l2_normalize(x):
    xf = x.astype(jnp.float32)
    return xf * lax.rsqrt(jnp.sum(xf * xf, axis=-1, keepdims=True) + EPS)


def causal_conv(x, w):
    k, ch = w.shape
    return lax.conv_general_dilated(x, w[:, None, :].astype(x.dtype), (1,), [(k - 1, 0)],
                                    dimension_numbers=('NWC', 'WIO', 'NWC'),
                                    feature_group_count=ch)


def rotary(x, positions):
    half = x.shape[-1] // 2
    inv_freq = ROPE_BASE ** (-jnp.arange(half, dtype=jnp.float32) / half)
    ang = positions.astype(jnp.float32)[..., None] * inv_freq
    cos, sin = jnp.cos(ang)[:, :, None, :], jnp.sin(ang)[:, :, None, :]
    x1, x2 = x[..., :half].astype(jnp.float32), x[..., half:].astype(jnp.float32)
    return jnp.concatenate([x1 * cos - x2 * sin, x2 * cos + x1 * sin], axis=-1).astype(x.dtype)


def to_chunks(x, size):
    b, t, h, d = x.shape
    return x.reshape(b, t // size, size, h, d).transpose(0, 3, 1, 2, 4)


def from_chunks(x):
    b, h, n, l, d = x.shape
    return x.transpose(0, 2, 3, 1, 4).reshape(b, n * l, h, d)


def gated_delta_rule(q, k, v, beta, g):
    L = CHUNK
    dk, dv = q.shape[-1], v.shape[-1]
    q, k, v = to_chunks(q, L) * dk ** -0.5, to_chunks(k, L), to_chunks(v, L)
    beta = to_chunks(beta[..., None], L)[..., 0]
    gc = jnp.cumsum(to_chunks(g[..., None], L)[..., 0], axis=-1)
    causal = jnp.tril(jnp.ones((L, L), dtype=bool))
    decay = jnp.exp(jnp.where(causal, gc[..., :, None] - gc[..., None, :], -jnp.inf))
    kb = k * beta[..., None]
    a_kk = jnp.einsum('bhnld,bhnsd->bhnls', kb, k) * decay
    lower = jnp.tril(a_kk, -1) + jnp.eye(L, dtype=a_kk.dtype)
    rhs = jnp.concatenate([v * beta[..., None], kb * jnp.exp(gc)[..., None]], axis=-1)
    sol = lax.linalg.triangular_solve(lower, rhs, left_side=True, lower=True, unit_diagonal=True)
    u, w = sol[..., :dv], sol[..., dv:]
    a_qk = jnp.einsum('bhnld,bhnsd->bhnls', q, k) * decay
    q_dec = q * jnp.exp(gc)[..., None]
    k_dec = k * jnp.exp(gc[..., -1:] - gc)[..., None]
    g_end = jnp.exp(gc[..., -1])

    def step(state, inp):
        u_c, w_c, q_c, k_c, a_c, ge = inp
        v_new = u_c - jnp.einsum('bhld,bhde->bhle', w_c, state)
        o_c = jnp.einsum('bhld,bhde->bhle', q_c, state) + jnp.einsum('bhls,bhse->bhle', a_c, v_new)
        state = state * ge[..., None, None] + jnp.einsum('bhld,bhle->bhde', k_c, v_new)
        return state, o_c

    xs = tuple(jnp.moveaxis(a, 2, 0) for a in (u, w, q_dec, k_dec, a_qk, g_end))
    state0 = jnp.zeros(q.shape[:2] + (dk, dv), q.dtype)
    _, o = lax.scan(step, state0, xs)
    return from_chunks(jnp.moveaxis(o, 0, 2))


def ssd_scan(x, dt, a_neg, bmat, cmat):
    b, t, h, p = x.shape
    g, n = bmat.shape[2], bmat.shape[3]
    r = h // g
    L = CHUNK
    nc = t // L
    xdt = (x * dt[..., None]).reshape(b, nc, L, g, r, p)
    acum = jnp.cumsum((dt * a_neg).reshape(b, nc, L, g, r), axis=2).transpose(0, 1, 3, 4, 2)
    bc = bmat.reshape(b, nc, L, g, n)
    cc = cmat.reshape(b, nc, L, g, n)
    causal = jnp.tril(jnp.ones((L, L), dtype=bool))
    seg = jnp.exp(jnp.where(causal, acum[..., :, None] - acum[..., None, :], -jnp.inf))
    cb = jnp.einsum('bclgn,bcsgn->bcgls', cc, bc)
    y_diag = jnp.einsum('bcgrls,bcsgrp->bclgrp', cb[:, :, :, None] * seg, xdt)
    to_end = jnp.exp(acum[..., -1:] - acum).transpose(0, 1, 4, 2, 3)
    states = jnp.einsum('bcsgn,bcsgrp->bcgrnp', bc, xdt * to_end[..., None])
    chunk_decay = jnp.exp(acum[..., -1])

    def step(hs, inp):
        st, dc = inp
        return hs * dc[..., None, None] + st, hs

    h0 = jnp.zeros((b, g, r, n, p), x.dtype)
    _, h_prev = lax.scan(step, h0, (jnp.moveaxis(states, 1, 0), jnp.moveaxis(chunk_decay, 1, 0)))
    h_prev = jnp.moveaxis(h_prev, 0, 1)
    from_start = jnp.exp(acum).transpose(0, 1, 4, 2, 3)
    y_off = jnp.einsum('bclgn,bcgrnp->bclgrp', cc, h_prev) * from_start[..., None]
    return (y_diag + y_off).reshape(b, t, h, p)


def retention(q, k, v, log_gamma):
    L = CHUNK
    dk = q.shape[-1]
    q, k, v = to_chunks(q, L), to_chunks(k, L) * dk ** -0.5, to_chunks(v, L)
    pos = jnp.arange(L, dtype=jnp.float32)
    causal = jnp.tril(jnp.ones((L, L), dtype=bool))
    lg = log_gamma[:, None]
    dmat = jnp.exp(jnp.where(causal, (pos[:, None] - pos[None, :]) * log_gamma[:, None, None], -jnp.inf))
    o_intra = jnp.einsum('bhnls,bhnse->bhnle', jnp.einsum('bhnld,bhnsd->bhnls', q, k) * dmat[:, None], v)
    zeta = jnp.exp((L - 1 - pos) * lg)
    xi = jnp.exp((pos + 1) * lg)
    g_chunk = jnp.exp(L * log_gamma)
    kv = jnp.einsum('bhnsd,bhnse->bhnde', k * zeta[:, None, :, None], v)

    def step(rstate, kv_c):
        return rstate * g_chunk[:, None, None] + kv_c, rstate

    r0 = jnp.zeros(q.shape[:2] + (dk, v.shape[-1]), q.dtype)
    _, r_prev = lax.scan(step, r0, jnp.moveaxis(kv, 2, 0))
    r_prev = jnp.moveaxis(r_prev, 0, 2)
    o_inter = jnp.einsum('bhnld,bhnde->bhnle', q, r_prev) * xi[:, None, :, None]
    return from_chunks(o_intra + o_inter)


def gla_chunked(q, k, v, log_a):
    L = GLA_CHUNK
    dk = q.shape[-1]
    q, k, v, log_a = to_chunks(q, L) * dk ** -0.5, to_chunks(k, L), to_chunks(v, L), to_chunks(log_a, L)
    bcum = jnp.cumsum(log_a, axis=3)
    causal = jnp.tril(jnp.ones((L, L), dtype=bool))[..., None]
    pair = jnp.exp(jnp.where(causal, bcum[..., :, None, :] - bcum[..., None, :, :], -jnp.inf))
    a_intra = jnp.einsum('bhnlsd,bhnsd->bhnls', pair * q[..., :, None, :], k)
    o_intra = jnp.einsum('bhnls,bhnse->bhnle', a_intra, v)
    q_dec = q * jnp.exp(bcum)
    k_dec = k * jnp.exp(bcum[..., -1:, :] - bcum)
    a_end = jnp.exp(bcum[..., -1, :])

    def step(state, inp):
        q_c, k_c, v_c, a_c = inp
        o_c = jnp.einsum('bhld,bhde->bhle', q_c, state)
        state = state * a_c[..., None] + jnp.einsum('bhld,bhle->bhde', k_c, v_c)
        return state, o_c

    xs = tuple(jnp.moveaxis(a, 2, 0) for a in (q_dec, k_dec, v, a_end))
    state0 = jnp.zeros(q.shape[:2] + (dk, v.shape[-1]), q.dtype)
    _, o_inter = lax.scan(step, state0, xs)
    return from_chunks(o_intra + jnp.moveaxis(o_inter, 0, 2))


def hybrid_mixer(h, positions, w_in, gdn_conv_w, gdn_a_log, gdn_dt_bias, gdn_norm_w,
                 ssd_conv_w, ssd_conv_b, ssd_a_log, ssd_dt_bias, ssd_d, ssd_norm_w,
                 gla_gate_w2, gla_gate_b2, gla_norm_w, w_branch, w_out):
    f32 = jnp.float32
    b, t, _ = h.shape
    (gdn_qkv, gdn_b, gdn_a, gdn_gate, ssd_z, ssd_xbc, ssd_dt,
     ret_q, ret_k, ret_v, ret_gate, gla_q, gla_k, gla_v, gla_r, gla_gate,
     merge) = jnp.split(h @ w_in, np.cumsum(IN_SPLITS)[:-1].tolist(), axis=-1)

    qkv = jax.nn.silu(causal_conv(gdn_qkv, gdn_conv_w))
    q, k, v = jnp.split(qkv, [GDN_QK, 2 * GDN_QK], axis=-1)
    q = l2_normalize(q.reshape(b, t, GDN_HEADS, GDN_DK))
    k = l2_normalize(k.reshape(b, t, GDN_HEADS, GDN_DK))
    v = v.reshape(b, t, GDN_HEADS, GDN_DV).astype(f32)
    beta = jax.nn.sigmoid(gdn_b.astype(f32))
    log_decay = -jnp.exp(gdn_a_log.astype(f32)) * jax.nn.softplus(gdn_a.astype(f32) + gdn_dt_bias.astype(f32))
    o = gated_delta_rule(q, k, v, beta, log_decay).astype(h.dtype)
    out_a = (rms_norm(o, gdn_norm_w) * jax.nn.silu(gdn_gate.reshape(b, t, GDN_HEADS, GDN_DV))).reshape(b, t, BRANCH_WIDTH)

    xbc = jax.nn.silu(causal_conv(ssd_xbc, ssd_conv_w) + ssd_conv_b)
    xs, bm, cm = jnp.split(xbc, [BRANCH_WIDTH, BRANCH_WIDTH + SSD_BC], axis=-1)
    xs = xs.reshape(b, t, SSD_HEADS, SSD_HEAD_DIM)
    dt = jax.nn.softplus(ssd_dt.astype(f32) + ssd_dt_bias.astype(f32))
    y = ssd_scan(xs.astype(f32), dt, -jnp.exp(ssd_a_log.astype(f32)),
                 bm.reshape(b, t, SSD_GROUPS, SSD_STATE).astype(f32),
                 cm.reshape(b, t, SSD_GROUPS, SSD_STATE).astype(f32))
    y = (y.astype(h.dtype) + xs * ssd_d[:, None]).reshape(b, t, BRANCH_WIDTH) * jax.nn.silu(ssd_z)
    out_b = rms_norm(y.reshape(b, t, SSD_GROUPS, -1), ssd_norm_w.reshape(SSD_GROUPS, -1)).reshape(b, t, BRANCH_WIDTH)

    log_gamma = jnp.log1p(-jnp.exp2(-5.0 - jnp.arange(RET_HEADS, dtype=f32)))
    q = rotary(ret_q.reshape(b, t, RET_HEADS, RET_DK), positions).astype(f32)
    k = rotary(ret_k.reshape(b, t, RET_HEADS, RET_DK), positions).astype(f32)
    o = retention(q, k, ret_v.reshape(b, t, RET_HEADS, RET_DV).astype(f32), log_gamma).astype(h.dtype)
    out_c = (rms_norm(o) * jax.nn.silu(ret_gate.reshape(b, t, RET_HEADS, RET_DV))).reshape(b, t, BRANCH_WIDTH)

    log_a = jax.nn.log_sigmoid((gla_r @ gla_gate_w2 + gla_gate_b2).astype(f32)) / GLA_TAU
    o = gla_chunked(gla_q.reshape(b, t, GLA_HEADS, GLA_DK).astype(f32),
                    gla_k.reshape(b, t, GLA_HEADS, GLA_DK).astype(f32),
                    gla_v.reshape(b, t, GLA_HEADS, GLA_DV).astype(f32),
                    log_a.reshape(b, t, GLA_HEADS, GLA_DK)).astype(h.dtype)
    out_d = (rms_norm(o, gla_norm_w) * jax.nn.silu(gla_gate.reshape(b, t, GLA_HEADS, GLA_DV))).reshape(b, t, BRANCH_WIDTH)

    branches = jnp.stack([out_a, out_b, out_c, out_d], axis=2)
    gates = jax.nn.sigmoid(merge.reshape(b, t, N_BRANCH, D_MODEL))
    merged = jnp.sum(gates * jnp.einsum('btiw,iwd->btid', branches, w_branch), axis=2)
    return merged @ w_out


def modulate(x, shift, scale):
    return x * (1 + scale[:, None, :]) + shift[:, None, :]


def swiglu(h, w_gate, w_up, w_down):
    return (jax.nn.silu(h @ w_gate) * (h @ w_up)) @ w_down


def moe_swiglu(h, w_router, w_gate, w_up, w_down):
    probs = jax.nn.softmax((h @ w_router).astype(jnp.float32), axis=-1)
    top_p, top_i = lax.top_k(probs, TOP_K)
    top_p = top_p / jnp.sum(top_p, axis=-1, keepdims=True)
    combine = jnp.sum(jax.nn.one_hot(top_i, N_EXPERTS, dtype=jnp.float32) * top_p[..., None], axis=-2).astype(h.dtype)
    out = jnp.zeros_like(h)
    for e in range(N_EXPERTS):
        out = out + combine[..., e:e + 1] * swiglu(h, w_gate[e], w_up[e], w_down[e])
    return out


def setup_inputs(seed: int = 0) -> dict:
    key = jax.random.key(seed)
    ks = list(jax.random.split(key, 32))
    f32 = jnp.float32
    n_dense, n_moe = (DEPTH + 1) // 2, DEPTH // 2
    xbc_ch = BRANCH_WIDTH + 2 * SSD_BC

    def normal(k, shape, scale):
        return scale * jax.random.normal(k, shape, f32)

    def gain(k, shape, s=0.02):
        return 1.0 + s * jax.random.normal(k, shape, f32)

    def a_log(k, shape):
        return jnp.log(jax.random.uniform(k, shape, f32, 1.0, 16.0))

    def dt_bias(k, shape):
        dt = jnp.exp(jax.random.uniform(k, shape, f32, math.log(1e-3), math.log(1e-1)))
        return dt + jnp.log(-jnp.expm1(-dt))

    return {
        'x': normal(ks[0], (BATCH, SEQ, D_MODEL), 1.0),
        'c': normal(ks[1], (BATCH, D_MODEL), 1.0),
        'positions': jnp.broadcast_to(jnp.arange(SEQ, dtype=jnp.int32), (BATCH, SEQ)),
        'ada_w': normal(ks[2], (DEPTH, D_MODEL, 6 * D_MODEL), 0.5 * D_MODEL ** -0.5),
        'ada_b': normal(ks[3], (DEPTH, 6 * D_MODEL), 0.02),
        'norm_mix': gain(ks[4], (DEPTH, D_MODEL)),
        'norm_ffn': gain(ks[5], (DEPTH, D_MODEL)),
        'w_in': normal(ks[6], (DEPTH, D_MODEL, IN_TOTAL), D_MODEL ** -0.5),
        'gdn_conv_w': normal(ks[7], (DEPTH, CONV_WIDTH, 2 * GDN_QK + BRANCH_WIDTH), CONV_WIDTH ** -0.5),
        'gdn_a_log': a_log(ks[8], (DEPTH, GDN_HEADS)),
        'gdn_dt_bias': dt_bias(ks[9], (DEPTH, GDN_HEADS)),
        'gdn_norm_w': gain(ks[10], (DEPTH, GDN_DV)),
        'ssd_conv_w': normal(ks[11], (DEPTH, CONV_WIDTH, xbc_ch), CONV_WIDTH ** -0.5),
        'ssd_conv_b': normal(ks[12], (DEPTH, xbc_ch), 0.02),
        'ssd_a_log': a_log(ks[13], (DEPTH, SSD_HEADS)),
        'ssd_dt_bias': dt_bias(ks[14], (DEPTH, SSD_HEADS)),
        'ssd_d': gain(ks[15], (DEPTH, SSD_HEADS), 0.1),
        'ssd_norm_w': gain(ks[16], (DEPTH, BRANCH_WIDTH)),
        'gla_gate_w2': normal(ks[17], (DEPTH, GLA_GATE_RANK, GLA_QK), GLA_GATE_RANK ** -0.5),
        'gla_gate_b2': normal(ks[18], (DEPTH, GLA_QK), 0.02),
        'gla_norm_w': gain(ks[19], (DEPTH, GLA_DV)),
        'w_branch': normal(ks[20], (DEPTH, N_BRANCH, BRANCH_WIDTH, D_MODEL), BRANCH_WIDTH ** -0.5),
        'w_out': normal(ks[21], (DEPTH, D_MODEL, D_MODEL), D_MODEL ** -0.5),
        'ffn_w_gate': normal(ks[22], (n_dense, D_MODEL, D_FF), D_MODEL ** -0.5),
        'ffn_w_up': normal(ks[23], (n_dense, D_MODEL, D_FF), D_MODEL ** -0.5),
        'ffn_w_down': normal(ks[24], (n_dense, D_FF, D_MODEL), D_FF ** -0.5),
        'moe_router': normal(ks[25], (n_moe, D_MODEL, N_EXPERTS), D_MODEL ** -0.5),
        'moe_w_gate': normal(ks[26], (n_moe, N_EXPERTS, D_MODEL, D_FF_EXPERT), D_MODEL ** -0.5),
        'moe_w_up': normal(ks[27], (n_moe, N_EXPERTS, D_MODEL, D_FF_EXPERT), D_MODEL ** -0.5),
        'moe_w_down': normal(ks[28], (n_moe, N_EXPERTS, D_FF_EXPERT, D_MODEL), D_FF_EXPERT ** -0.5),
        'norm_final': gain(ks[29], (D_MODEL,)),
    }


def reference(x, c, positions, ada_w, ada_b, norm_mix, norm_ffn, w_in, gdn_conv_w, gdn_a_log,
              gdn_dt_bias, gdn_norm_w, ssd_conv_w, ssd_conv_b, ssd_a_log, ssd_dt_bias, ssd_d,
              ssd_norm_w, gla_gate_w2, gla_gate_b2, gla_norm_w, w_branch, w_out, ffn_w_gate,
              ffn_w_up, ffn_w_down, moe_router, moe_w_gate, moe_w_up, moe_w_down, norm_final):
    cond = jax.nn.silu(c)
    for i in range(DEPTH):
        mod = cond @ ada_w[i] + ada_b[i]
        shift_m, scale_m, gate_m, shift_f, scale_f, gate_f = jnp.split(mod, 6, axis=-1)
        h = modulate(rms_norm(x, norm_mix[i]), shift_m, scale_m)
        x = x + gate_m[:, None, :] * hybrid_mixer(
            h, positions, w_in[i], gdn_conv_w[i], gdn_a_log[i], gdn_dt_bias[i], gdn_norm_w[i],
            ssd_conv_w[i], ssd_conv_b[i], ssd_a_log[i], ssd_dt_bias[i], ssd_d[i], ssd_norm_w[i],
            gla_gate_w2[i], gla_gate_b2[i], gla_norm_w[i], w_branch[i], w_out[i])
        h = modulate(rms_norm(x, norm_ffn[i]), shift_f, scale_f)
        j = i // 2
        if i % 2 == 0:
            f = swiglu(h, ffn_w_gate[j], ffn_w_up[j], ffn_w_down[j])
        else:
            f = moe_swiglu(h, moe_router[j], moe_w_gate[j], moe_w_up[j], moe_w_down[j])
        x = x + gate_f[:, None, :] * f
    return rms_norm(x, norm_final)
```

```python
import functools
import math

import jax
import jax.numpy as jnp
import numpy as np
from jax import lax
from jax.experimental import pallas as pl
from jax.experimental.pallas import tpu as pltpu

F32 = jnp.float32
BF16 = jnp.bfloat16
HI = lax.Precision.HIGHEST

D_MODEL = 2048
DEPTH = 4
BRANCH_WIDTH = D_MODEL // 2
N_BRANCH = 4
CONV_WIDTH = 4
GDN_HEADS, GDN_DK = 8, 128
SSD_HEADS, SSD_HEAD_DIM, SSD_GROUPS, SSD_STATE = 16, 64, 2, 128
RET_HEADS, RET_DK = 8, 128
GLA_HEADS, GLA_DK, GLA_DV = 4, 128, 256
GLA_GATE_RANK = 16
GLA_TAU = 16.0
D_FF = 11 * D_MODEL // 4
N_EXPERTS = 8
ROPE_BASE = 10000.0
EPS = 1e-6

LANES = 128
VMEM_LIMIT = 56 * 1024 * 1024

OFF_GDN_Q, OFF_GDN_K, OFF_GDN_V = 0, 1024, 2048
OFF_GDN_GATE = 3072
OFF_SSD_Z = 4096
OFF_SSD_X, OFF_SSD_B, OFF_SSD_C = 5120, 6144, 6400
OFF_RET_Q, OFF_RET_K, OFF_RET_V, OFF_RET_GATE = 6656, 7680, 8704, 9728
OFF_GLA_Q, OFF_GLA_K, OFF_GLA_V, OFF_GLA_GATE = 10752, 11264, 11776, 12800
OFF_MERGE = 13824
OFF_SMALL = 22016
NP = 22528
SM_B, SM_A, SM_DT, SM_R = 0, 8, 16, 32

CHUNK = 64
SUB = 16
ROW_TILE = 256
TOK_BLOCK = 512
DISPATCH_TOK = 1024
COMBINE_ROWS = 512


def _cparams(*sem):
    return pltpu.CompilerParams(dimension_semantics=sem, vmem_limit_bytes=VMEM_LIMIT)


def _silu(x):
    return x * jax.nn.sigmoid(x)


def _softplus(x):
    return jnp.maximum(x, 0.0) + jnp.log1p(jnp.exp(-jnp.abs(x)))


def _mm(a, b):
    return jnp.dot(a.astype(BF16), b.astype(BF16), preferred_element_type=F32)


def _mm_nt(a, b):
    return lax.dot_general(a.astype(BF16), b.astype(BF16), (((1,), (1,)), ((), ())),
                           preferred_element_type=F32)


def _mm_tn(a, b):
    return lax.dot_general(a.astype(BF16), b.astype(BF16), (((0,), (0,)), ((), ())),
                           preferred_element_type=F32)


def _mm_hi(a, b):
    return jnp.dot(a, b, precision=HI, preferred_element_type=F32)


def _tri(n, strict=False):
    r = lax.broadcasted_iota(jnp.int32, (n, n), 0)
    c = lax.broadcasted_iota(jnp.int32, (n, n), 1)
    return (r > c) if strict else (r >= c)


def _lane_pick(x, lane_idx):
    lane = lax.broadcasted_iota(jnp.int32, x.shape, 1)
    return jnp.sum(jnp.where(lane == lane_idx, x, 0.0), axis=1, keepdims=True)


def _rms(x, w=None):
    y = x * lax.rsqrt(jnp.mean(x * x, axis=-1, keepdims=True) + EPS)
    return y if w is None else y * w


def _causal_conv_silu(raw_ref, buf, cw_ref, bias=None):
    tb = raw_ref.shape[0]
    buf[8:8 + tb, :] = raw_ref[...].astype(F32)
    acc = buf[5:5 + tb, :] * cw_ref[0:1, :]
    for i in range(1, CONV_WIDTH):
        acc = acc + buf[5 + i:5 + i + tb, :] * cw_ref[i:i + 1, :]
    buf[0:8, :] = buf[tb:tb + 8, :]
    if bias is not None:
        acc = acc + bias
    return _silu(acc)


def _ada_kernel(c_ref, w_ref, b_ref, o_ref):
    c = c_ref[...]
    o_ref[0] = jnp.sum(w_ref[0] * _silu(c), axis=0, keepdims=True) + b_ref[0]


def _ada_mod(c, ada_w, ada_b):
    depth, d, n = ada_w.shape
    tn = 512
    return pl.pallas_call(
        _ada_kernel,
        grid=(depth, n // tn),
        in_specs=[pl.BlockSpec((d, 1), lambda i, j: (0, 0)),
                  pl.BlockSpec((1, d, tn), lambda i, j: (i, 0, j)),
                  pl.BlockSpec((1, 1, tn), lambda i, j: (i, 0, j))],
        out_specs=pl.BlockSpec((1, 1, tn), lambda i, j: (i, 0, j)),
        out_shape=jax.ShapeDtypeStruct((depth, 1, n), F32),
        compiler_params=_cparams("parallel", "parallel"),
        name="ada_mod",
    )(c.reshape(d, 1), ada_w, ada_b.reshape(depth, 1, n))


def _rope_kernel(pos_ref, freq_ref, cos_ref, sin_ref):
    ang = pos_ref[...].astype(F32) * freq_ref[...]
    lane = lax.broadcasted_iota(jnp.int32, ang.shape, 1)
    cos_ref[...] = jnp.cos(ang)
    sin_ref[...] = jnp.where(lane < RET_DK // 2, -1.0, 1.0) * jnp.sin(ang)


def _rope_tables(positions):
    t = positions.shape[-1]
    half = RET_DK // 2
    inv_freq = ROPE_BASE ** (-jnp.arange(half, dtype=F32) / half)
    freq = jnp.concatenate([inv_freq, inv_freq]).reshape(1, RET_DK)
    tb = min(t, 1024)
    return pl.pallas_call(
        _rope_kernel,
        grid=(t // tb,),
        in_specs=[pl.BlockSpec((tb, 1), lambda i: (i, 0)),
                  pl.BlockSpec((1, RET_DK), lambda i: (0, 0))],
        out_specs=[pl.BlockSpec((tb, RET_DK), lambda i: (i, 0))] * 2,
        out_shape=[jax.ShapeDtypeStruct((t, RET_DK), F32)] * 2,
        compiler_params=_cparams("parallel"),
        name="rope_tables",
    )(positions.reshape(t, 1), freq)


def _norm_mod(xf, nw, shift, scale):
    return _rms(xf, nw) * (1.0 + scale) + shift


def _prenorm_kernel(x_ref, nw_ref, sh_ref, sc_ref, h_ref):
    h_ref[...] = _norm_mod(x_ref[...], nw_ref[...], sh_ref[...], sc_ref[...]).astype(h_ref.dtype)


def _prenorm(x, nw, shift, scale):
    t, d = x.shape
    tm = min(t, 512)
    vec = pl.BlockSpec((1, d), lambda i: (0, 0))
    return pl.pallas_call(
        _prenorm_kernel,
        grid=(t // tm,),
        in_specs=[pl.BlockSpec((tm, d), lambda i: (i, 0)), vec, vec, vec],
        out_specs=pl.BlockSpec((tm, d), lambda i: (i, 0)),
        out_shape=jax.ShapeDtypeStruct((t, d), BF16),
        compiler_params=_cparams("parallel"),
        name="prenorm",
    )(x, nw, shift, scale)


def _residual_kernel(x_ref, y_ref, g_ref, nw_ref, sh_ref, sc_ref, *rest, mode):
    xn = x_ref[...] + g_ref[...] * y_ref[...]
    if mode == "final":
        (o_ref,) = rest
        o_ref[...] = _rms(xn, nw_ref[...])
        return
    hf = _norm_mod(xn, nw_ref[...], sh_ref[...], sc_ref[...])
    if mode == "router":
        wr_ref, xo_ref, h_ref, lg_ref = rest
        lg_ref[...] = _mm_hi(hf, wr_ref[...])
    else:
        xo_ref, h_ref = rest
    xo_ref[...] = xn
    h_ref[...] = hf.astype(h_ref.dtype)


def _residual(x, y, gate, nw, shift, scale, mode="plain", w_router=None):
    t, d = x.shape
    tm = min(t, 256)
    row = pl.BlockSpec((tm, d), lambda i: (i, 0))
    vec = pl.BlockSpec((1, d), lambda i: (0, 0))
    in_specs = [row, row, vec, vec, vec, vec]
    args = [x, y, gate, nw, shift, scale]
    if mode == "final":
        out_specs, out_shape = row, jax.ShapeDtypeStruct((t, d), F32)
    else:
        out_specs = [row, row]
        out_shape = [jax.ShapeDtypeStruct((t, d), F32), jax.ShapeDtypeStruct((t, d), BF16)]
        if mode == "router":
            in_specs.append(pl.BlockSpec((d, LANES), lambda i: (0, 0)))
            args.append(w_router)
            out_specs.append(pl.BlockSpec((tm, LANES), lambda i: (i, 0)))
            out_shape.append(jax.ShapeDtypeStruct((t, LANES), F32))
    return pl.pallas_call(
        functools.partial(_residual_kernel, mode=mode),
        grid=(t // tm,),
        in_specs=in_specs, out_specs=out_specs, out_shape=out_shape,
        compiler_params=_cparams("parallel"),
        name="residual_" + mode,
    )(*args)


def _matmul_kernel(a_ref, w_ref, o_ref):
    o_ref[...] = jnp.dot(a_ref[...], w_ref[...], preferred_element_type=F32).astype(o_ref.dtype)


def _matmul(a, w, out_dtype, tm, tn, name):
    m, k = a.shape
    n = w.shape[1]
    tm, tn = min(tm, m), min(tn, n)
    return pl.pallas_call(
        _matmul_kernel,
        grid=(n // tn, m // tm),
        in_specs=[pl.BlockSpec((tm, k), lambda j, i: (i, 0)),
                  pl.BlockSpec((k, tn), lambda j, i: (0, j))],
        out_specs=pl.BlockSpec((tm, tn), lambda j, i: (i, j)),
        out_shape=jax.ShapeDtypeStruct((m, n), out_dtype),
        compiler_params=_cparams("parallel", "parallel"),
        name=name,
    )(a, w)


def _gdn_kernel(q_ref, k_ref, v_ref, gate_ref, sm_ref, cq_ref, ck_ref, cv_ref, alog_ref, dtb_ref,
                nw_ref, o_ref, qbuf, kbuf, vbuf, state):
    h = pl.program_id(0)
    tb = q_ref.shape[0]

    @pl.when(pl.program_id(1) == 0)
    def _():
        zero = jnp.zeros((8, GDN_DK), F32)
        qbuf[0:8, :] = zero
        kbuf[0:8, :] = zero
        vbuf[0:8, :] = zero
        state[...] = jnp.zeros_like(state)

    q = _causal_conv_silu(q_ref, qbuf, cq_ref)
    k = _causal_conv_silu(k_ref, kbuf, ck_ref)
    v = _causal_conv_silu(v_ref, vbuf, cv_ref)
    q = q * lax.rsqrt(jnp.sum(q * q, axis=-1, keepdims=True) + EPS) * (GDN_DK ** -0.5)
    k = k * lax.rsqrt(jnp.sum(k * k, axis=-1, keepdims=True) + EPS)

    sm = sm_ref[...].astype(F32)
    beta = _lane_pick(jax.nn.sigmoid(sm), SM_B + h)
    g_all = -jnp.exp(alog_ref[...]) * _softplus(sm + dtb_ref[...])
    g = _lane_pick(g_all, SM_A + h)

    L = CHUNK
    tril = _tri(L)
    tril_f = tril.astype(F32)
    strict = _tri(L, strict=True)
    eye = (lax.broadcasted_iota(jnp.int32, (L, L), 0)
           == lax.broadcasted_iota(jnp.int32, (L, L), 1)).astype(F32)
    nw = nw_ref[...]
    s = state[...]
    for c in range(tb // L):
        sl = slice(c * L, (c + 1) * L)
        qc, kc, vc, bc = q[sl], k[sl], v[sl], beta[sl]
        gcb = _mm_hi(tril_f, jnp.broadcast_to(g[sl], (L, GDN_DK)))
        gc_col = gcb[:, 0:L]
        gc_row = gcb.T[0:L, :]
        decay = jnp.exp(jnp.where(tril, gc_col - gc_row, -jnp.inf))
        kb = kc * bc
        a_kk = _mm_nt(kb, kc) * decay
        n_mat = jnp.where(strict, -a_kk, 0.0)
        t_inv = eye + n_mat
        pw = n_mat
        for _ in range(5):
            pw = _mm(pw, pw)
            t_inv = t_inv + _mm(t_inv, pw)
        eg = jnp.exp(gcb)
        u = _mm(t_inv, vc * bc)
        w = _mm(t_inv, kb * eg)
        a_qk = _mm_nt(qc, kc) * decay
        g_last = gcb[L - 1:L, :]
        k_dec = kc * jnp.exp(g_last - gcb)
        v_new = u - _mm(w, s)
        o = _mm(qc * eg, s) + _mm(a_qk, v_new)
        s = s * jnp.exp(g_last) + _mm_tn(k_dec, v_new)
        out = _rms(o, nw) * _silu(gate_ref[sl, :].astype(F32))
        o_ref[sl, :] = out.astype(o_ref.dtype)
    state[...] = s


def _gdn_branch(p, conv_w, a_log, dt_bias, norm_w, tb):
    t = p.shape[0]
    nb = lambda off: off // GDN_DK
    col = lambda off: pl.BlockSpec((tb, GDN_DK), lambda h, i, o=nb(off): (i, o + h))
    cw = lambda off: pl.BlockSpec((CONV_WIDTH, GDN_DK), lambda h, i, o=nb(off): (0, o + h))
    vec = pl.BlockSpec((1, LANES), lambda h, i: (0, 0))
    pad = lambda x, off: jnp.zeros((1, LANES), F32).at[0, off:off + x.shape[0]].set(x)
    return pl.pallas_call(
        _gdn_kernel,
        grid=(GDN_HEADS, t // tb),
        in_specs=[col(OFF_GDN_Q), col(OFF_GDN_K), col(OFF_GDN_V), col(OFF_GDN_GATE),
                  pl.BlockSpec((tb, LANES), lambda h, i: (i, OFF_SMALL // LANES)),
                  cw(0), cw(1024), cw(2048), vec, vec, vec],
        out_specs=pl.BlockSpec((tb, GDN_DK), lambda h, i: (i, h)),
        out_shape=jax.ShapeDtypeStruct((t, BRANCH_WIDTH), BF16),
        scratch_shapes=[pltpu.VMEM((tb + 8, GDN_DK), F32)] * 3 + [pltpu.VMEM((GDN_DK, GDN_DK), F32)],
        compiler_params=_cparams("parallel", "arbitrary"),
        name="gdn_branch",
    )(p, p, p, p, p, conv_w, conv_w, conv_w, pad(a_log, SM_A), pad(dt_bias, SM_A),
      norm_w.reshape(1, GDN_DK))


def _ssd_kernel(x_ref, b_ref, c_ref, z_ref, sm_ref, cwx_ref, cwb_ref, cwc_ref, cbx_ref, cbb_ref,
                cbc_ref, alog_ref, dtb_ref, dskip_ref, nw_ref, o_ref, xbuf, bbuf, cbuf, state):
    g = pl.program_id(0)
    tb = x_ref.shape[0]
    hpg = SSD_HEADS // SSD_GROUPS
    gw = hpg * SSD_HEAD_DIM

    @pl.when(pl.program_id(1) == 0)
    def _():
        xbuf[0:8, :] = jnp.zeros((8, gw), F32)
        bbuf[0:8, :] = jnp.zeros((8, SSD_STATE), F32)
        cbuf[0:8, :] = jnp.zeros((8, SSD_STATE), F32)
        state[...] = jnp.zeros_like(state)

    xs = _causal_conv_silu(x_ref, xbuf, cwx_ref, cbx_ref[...])
    bm = _causal_conv_silu(b_ref, bbuf, cwb_ref, cbb_ref[...])
    cm = _causal_conv_silu(c_ref, cbuf, cwc_ref, cbc_ref[...])
    sm = sm_ref[...].astype(F32)
    dt_all = _softplus(sm + dtb_ref[...])
    da_all = dt_all * (-jnp.exp(alog_ref[...]))

    L = CHUNK
    tril = _tri(L)
    tril_f = tril.astype(F32)
    lane_lo = lax.broadcasted_iota(jnp.int32, (L, LANES), 1) < SSD_HEAD_DIM
    sub = lax.broadcasted_iota(jnp.int32, (LANES, L), 0)
    nw = nw_ref[...]
    dskip = dskip_ref[...]
    hs = [state[p] for p in range(hpg // 2)]
    for c in range(tb // L):
        sl = slice(c * L, (c + 1) * L)
        xc, bc, cc = xs[sl], bm[sl], cm[sl]
        acum = _mm_hi(tril_f, da_all[sl])
        acum_t = acum.T
        cb = _mm_nt(cc, bc)
        ys = []
        for p in range(hpg // 2):
            cols, segs, dts = [], [], []
            for r in (2 * p, 2 * p + 1):
                ln = SM_DT + g * hpg + r
                col = _lane_pick(acum, ln)
                row = jnp.sum(jnp.where(sub == ln, acum_t, 0.0), axis=0, keepdims=True)
                cols.append(col)
                segs.append(jnp.exp(jnp.where(tril, col - row, -jnp.inf)))
                dts.append(_lane_pick(dt_all[sl], ln))
            xdt = xc[:, p * LANES:(p + 1) * LANES] * jnp.where(lane_lo, dts[0], dts[1])
            y_diag = jnp.where(lane_lo, _mm(cb * segs[0], xdt), _mm(cb * segs[1], xdt))
            ac = jnp.where(lane_lo, cols[0], cols[1])
            last = ac[L - 1:L, :]
            y_off = _mm(cc, hs[p]) * jnp.exp(ac)
            hs[p] = hs[p] * jnp.exp(last) + _mm_tn(bc, xdt * jnp.exp(last - ac))
            ys.append(y_diag + y_off)
        y = jnp.concatenate(ys, axis=1) + xc * dskip
        y = y * _silu(z_ref[sl, :].astype(F32))
        o_ref[sl, :] = _rms(y, nw).astype(o_ref.dtype)
    for p in range(hpg // 2):
        state[p] = hs[p]


def _ssd_branch(p, conv_w, conv_b, a_log, dt_bias, d_skip, norm_w, tb):
    t = p.shape[0]
    gw = (SSD_HEADS // SSD_GROUPS) * SSD_HEAD_DIM
    conv_b = conv_b.reshape(1, -1)
    xspec = lambda off: pl.BlockSpec((tb, gw), lambda g, i, o=off // gw: (i, o + g))
    nspec = lambda off: pl.BlockSpec((tb, SSD_STATE), lambda g, i, o=off // SSD_STATE: (i, o + g))
    vec = pl.BlockSpec((1, LANES), lambda g, i: (0, 0))
    pad = lambda x: jnp.zeros((1, LANES), F32).at[0, SM_DT:SM_DT + SSD_HEADS].set(x)
    return pl.pallas_call(
        _ssd_kernel,
        grid=(SSD_GROUPS, t // tb),
        in_specs=[xspec(OFF_SSD_X), nspec(OFF_SSD_B), nspec(OFF_SSD_C), xspec(OFF_SSD_Z),
                  pl.BlockSpec((tb, LANES), lambda g, i: (i, OFF_SMALL // LANES)),
                  pl.BlockSpec((CONV_WIDTH, gw), lambda g, i: (0, g)),
                  pl.BlockSpec((CONV_WIDTH, SSD_STATE), lambda g, i: (0, 1024 // SSD_STATE + g)),
                  pl.BlockSpec((CONV_WIDTH, SSD_STATE), lambda g, i: (0, 1280 // SSD_STATE + g)),
                  pl.BlockSpec((1, gw), lambda g, i: (0, g)),
                  pl.BlockSpec((1, SSD_STATE), lambda g, i: (0, 1024 // SSD_STATE + g)),
                  pl.BlockSpec((1, SSD_STATE), lambda g, i: (0, 1280 // SSD_STATE + g)),
                  vec, vec,
                  pl.BlockSpec((1, gw), lambda g, i: (0, g)),
                  pl.BlockSpec((1, gw), lambda g, i: (0, g))],
        out_specs=pl.BlockSpec((tb, gw), lambda g, i: (i, g)),
        out_shape=jax.ShapeDtypeStruct((t, BRANCH_WIDTH), BF16),
        scratch_shapes=[pltpu.VMEM((tb + 8, gw), F32), pltpu.VMEM((tb + 8, SSD_STATE), F32),
                        pltpu.VMEM((tb + 8, SSD_STATE), F32),
                        pltpu.VMEM((SSD_HEADS // SSD_GROUPS // 2, SSD_STATE, LANES), F32)],
        compiler_params=_cparams("parallel", "arbitrary"),
        name="ssd_branch",
    )(p, p, p, p, p, conv_w, conv_w, conv_w, conv_b, conv_b, conv_b, pad(a_log), pad(dt_bias),
      jnp.repeat(d_skip, SSD_HEAD_DIM).reshape(1, BRANCH_WIDTH), norm_w.reshape(1, BRANCH_WIDTH))


def _ret_kernel(q_ref, k_ref, v_ref, gate_ref, cos_ref, sin_ref, lg_ref, o_ref, state):
    tb = q_ref.shape[0]

    @pl.when(pl.program_id(1) == 0)
    def _():
        state[...] = jnp.zeros_like(state)

    cosf, sinf = cos_ref[...], sin_ref[...]
    rot = lambda x: x * cosf + pltpu.roll(x, RET_DK // 2, 1) * sinf
    q = rot(q_ref[...].astype(F32))
    k = rot(k_ref[...].astype(F32)) * (RET_DK ** -0.5)
    v = v_ref[...].astype(F32)

    L = CHUNK
    lg = lg_ref[0]
    li = lax.broadcasted_iota(jnp.int32, (L, L), 0)
    si = lax.broadcasted_iota(jnp.int32, (L, L), 1)
    dmat = jnp.exp(jnp.where(li >= si, (li - si).astype(F32) * lg[:, 0:L], -jnp.inf))
    pos = lax.broadcasted_iota(jnp.int32, (L, RET_DK), 0).astype(F32)
    xi = jnp.exp((pos + 1.0) * lg)
    zeta = jnp.exp((L - 1.0 - pos) * lg)
    g_chunk = jnp.exp(float(L) * lg)
    s = state[...]
    for c in range(tb // L):
        sl = slice(c * L, (c + 1) * L)
        qc, kc, vc = q[sl], k[sl], v[sl]
        o = _mm(_mm_nt(qc, kc) * dmat, vc) + _mm(qc, s) * xi
        s = s * g_chunk + _mm_tn(kc * zeta, vc)
        out = _rms(o) * _silu(gate_ref[sl, :].astype(F32))
        o_ref[sl, :] = out.astype(o_ref.dtype)
    state[...] = s


def _ret_branch(p, cosf, sinf, tb):
    t = p.shape[0]
    col = lambda off: pl.BlockSpec((tb, RET_DK), lambda h, i, o=off // RET_DK: (i, o + h))
    tab = pl.BlockSpec((tb, RET_DK), lambda h, i: (i, 0))
    log_gamma = np.log1p(-np.exp2(-5.0 - np.arange(RET_HEADS, dtype=np.float32))).astype(np.float32)
    lg = jnp.asarray(np.broadcast_to(log_gamma[:, None, None], (RET_HEADS, 1, LANES)))
    return pl.pallas_call(
        _ret_kernel,
        grid=(RET_HEADS, t // tb),
        in_specs=[col(OFF_RET_Q), col(OFF_RET_K), col(OFF_RET_V), col(OFF_RET_GATE), tab, tab,
                  pl.BlockSpec((1, 1, LANES), lambda h, i: (h, 0, 0))],
        out_specs=pl.BlockSpec((tb, RET_DK), lambda h, i: (i, h)),
        out_shape=jax.ShapeDtypeStruct((t, BRANCH_WIDTH), BF16),
        scratch_shapes=[pltpu.VMEM((RET_DK, RET_DK), F32)],
        compiler_params=_cparams("parallel", "arbitrary"),
        name="ret_branch",
    )(p, p, p, p, cosf, sinf, lg)


def _gla_kernel(q_ref, k_ref, v_ref, gate_ref, sm_ref, w2_ref, b2_ref, nw_ref, o_ref, state):
    tb = q_ref.shape[0]

    @pl.when(pl.program_id(1) == 0)
    def _():
        state[...] = jnp.zeros_like(state)

    q = q_ref[...].astype(F32) * (GLA_DK ** -0.5)
    k = k_ref[...].astype(F32)
    v = v_ref[...].astype(F32)
    pre = _mm_hi(sm_ref[...].astype(F32), w2_ref[...]) + b2_ref[...]
    log_a = -_softplus(-pre) * (1.0 / GLA_TAU)

    L, nsub = CHUNK, CHUNK // SUB
    tril_f = _tri(L).astype(F32)
    rin = lax.broadcasted_iota(jnp.int32, (nsub, SUB, GLA_DK), 1)
    nw = nw_ref[...]
    s = state[...]
    for c in range(tb // L):
        sl = slice(c * L, (c + 1) * L)
        qc, kc, vc = q[sl], k[sl], v[sl]
        b = _mm_hi(tril_f, log_a[sl])
        b_end = b[L - 1:L, :]
        o = _mm(qc * jnp.exp(b), s)
        offs = [jnp.zeros((SUB, GLA_DV), F32)]
        for i in range(1, nsub):
            ref_b = b[i * SUB - 1:i * SUB, :]
            qi = qc[i * SUB:(i + 1) * SUB] * jnp.exp(b[i * SUB:(i + 1) * SUB] - ref_b)
            kt = kc[0:i * SUB] * jnp.exp(ref_b - b[0:i * SUB])
            offs.append(_mm(_mm_nt(qi, kt), vc[0:i * SUB]))
        o = o + jnp.concatenate(offs, axis=0)
        q4 = qc.reshape(nsub, SUB, GLA_DK)
        k4 = kc.reshape(nsub, SUB, GLA_DK)
        b4 = b.reshape(nsub, SUB, GLA_DK)
        v4 = vc.reshape(nsub, SUB, GLA_DV)
        od = jnp.zeros((nsub, SUB, GLA_DV), F32)
        for j in range(SUB):
            pair = jnp.exp(jnp.where(rin >= j, b4 - b4[:, j:j + 1, :], -jnp.inf))
            a_j = jnp.sum(q4 * k4[:, j:j + 1, :] * pair, axis=-1, keepdims=True)
            od = od + a_j * v4[:, j:j + 1, :]
        o = o + od.reshape(L, GLA_DV)
        a_col = jnp.exp(jnp.broadcast_to(b_end, (GLA_DK, GLA_DK)).T)
        a_col = jnp.concatenate([a_col] * (GLA_DV // GLA_DK), axis=1)
        s = s * a_col + _mm_tn(kc * jnp.exp(b_end - b), vc)
        out = _rms(o, nw) * _silu(gate_ref[sl, :].astype(F32))
        o_ref[sl, :] = out.astype(o_ref.dtype)
    state[...] = s


def _gla_branch(p, gate_w2, gate_b2, norm_w, tb):
    t = p.shape[0]
    qk = lambda off: pl.BlockSpec((tb, GLA_DK), lambda h, i, o=off // GLA_DK: (i, o + h))
    vv = lambda off: pl.BlockSpec((tb, GLA_DV), lambda h, i, o=off // GLA_DV: (i, o + h))
    w2p = jnp.zeros((LANES, GLA_HEADS * GLA_DK), F32).at[SM_R:SM_R + GLA_GATE_RANK].set(gate_w2)
    return pl.pallas_call(
        _gla_kernel,
        grid=(GLA_HEADS, t // tb),
        in_specs=[qk(OFF_GLA_Q), qk(OFF_GLA_K), vv(OFF_GLA_V), vv(OFF_GLA_GATE),
                  pl.BlockSpec((tb, LANES), lambda h, i: (i, OFF_SMALL // LANES)),
                  pl.BlockSpec((LANES, GLA_DK), lambda h, i: (0, h)),
                  pl.BlockSpec((1, GLA_DK), lambda h, i: (0, h)),
                  pl.BlockSpec((1, GLA_DV), lambda h, i: (0, 0))],
        out_specs=pl.BlockSpec((tb, GLA_DV), lambda h, i: (i, h)),
        out_shape=jax.ShapeDtypeStruct((t, BRANCH_WIDTH), BF16),
        scratch_shapes=[pltpu.VMEM((GLA_DK, GLA_DV), F32)],
        compiler_params=_cparams("parallel", "arbitrary"),
        name="gla_branch",
    )(p, p, p, p, p, w2p, gate_b2.reshape(1, -1), norm_w.reshape(1, GLA_DV))


def _merge_kernel(oa_ref, ob_ref, oc_ref, od_ref, ga_ref, gb_ref, gc_ref, gd_ref, w_ref, o_ref):
    acc = None
    for i, (o_r, g_r) in enumerate(((oa_ref, ga_ref), (ob_ref, gb_ref), (oc_ref, gc_ref),
                                    (od_ref, gd_ref))):
        term = jax.nn.sigmoid(g_r[...].astype(F32)) * jnp.dot(
            o_r[...], w_ref[i], preferred_element_type=F32)
        acc = term if acc is None else acc + term
    o_ref[...] = acc.astype(o_ref.dtype)


def _merge(outs, p, w_branch):
    t = p.shape[0]
    tm, tn = min(t, 1024), 512
    bro = pl.BlockSpec((tm, BRANCH_WIDTH), lambda j, i: (i, 0))
    gsp = lambda b: pl.BlockSpec((tm, tn), lambda j, i, o=(OFF_MERGE + b * D_MODEL) // tn: (i, o + j))
    return pl.pallas_call(
        _merge_kernel,
        grid=(D_MODEL // tn, t // tm),
        in_specs=[bro] * 4 + [gsp(b) for b in range(N_BRANCH)]
        + [pl.BlockSpec((N_BRANCH, BRANCH_WIDTH, tn), lambda j, i: (0, 0, j))],
        out_specs=pl.BlockSpec((tm, tn), lambda j, i: (i, j)),
        out_shape=jax.ShapeDtypeStruct((t, D_MODEL), BF16),
        compiler_params=_cparams("parallel", "parallel"),
        name="branch_merge",
    )(*outs, p, p, p, p, w_branch)


def _ffn_a_kernel(te_ref, tv_ref, a_ref, wg_ref, wu_ref, o_ref):
    i = pl.program_id(1)

    @pl.when(tv_ref[i] == 1)
    def _():
        a = a_ref[...]
        g = jnp.dot(a, wg_ref[0], preferred_element_type=F32)
        u = jnp.dot(a, wu_ref[0], preferred_element_type=F32)
        o_ref[...] = (_silu(g) * u).astype(o_ref.dtype)

    @pl.when(tv_ref[i] == 0)
    def _():
        o_ref[...] = jnp.zeros_like(o_ref)


def _ffn_a(a, w_gate, w_up, tile_expert, tile_valid, tm):
    m, d = a.shape
    f = w_gate.shape[-1]
    tf = 512
    wspec = pl.BlockSpec((1, d, tf), lambda j, i, te, tv: (te[i], 0, j))
    return pl.pallas_call(
        _ffn_a_kernel,
        grid_spec=pltpu.PrefetchScalarGridSpec(
            num_scalar_prefetch=2, grid=(f // tf, m // tm),
            in_specs=[pl.BlockSpec((tm, d), lambda j, i, te, tv: (i, 0)), wspec, wspec],
            out_specs=pl.BlockSpec((tm, tf), lambda j, i, te, tv: (i, j))),
        out_shape=jax.ShapeDtypeStruct((m, f), BF16),
        compiler_params=_cparams("parallel", "arbitrary"),
        name="ffn_gate_up",
    )(tile_expert, tile_valid, a, w_gate, w_up)


def _ffn_b_kernel(te_ref, tv_ref, a_ref, w_ref, *rest, scaled):
    i = pl.program_id(1)
    o_ref = rest[-1]

    @pl.when(tv_ref[i] == 1)
    def _():
        y = jnp.dot(a_ref[...], w_ref[0], preferred_element_type=F32)
        if scaled:
            y = y * rest[0][...]
        o_ref[...] = y.astype(o_ref.dtype)

    @pl.when(tv_ref[i] == 0)
    def _():
        o_ref[...] = jnp.zeros_like(o_ref)


def _ffn_b(act, w_down, tile_expert, tile_valid, tm, row_scale=None, out_dtype=F32):
    m, f = act.shape
    d = w_down.shape[-1]
    tn = 512
    in_specs = [pl.BlockSpec((tm, f), lambda j, i, te, tv: (i, 0)),
                pl.BlockSpec((1, f, tn), lambda j, i, te, tv: (te[i], 0, j))]
    args = [act, w_down]
    if row_scale is not None:
        in_specs.append(pl.BlockSpec((tm, 1), lambda j, i, te, tv: (i, 0)))
        args.append(row_scale)
    return pl.pallas_call(
        functools.partial(_ffn_b_kernel, scaled=row_scale is not None),
        grid_spec=pltpu.PrefetchScalarGridSpec(
            num_scalar_prefetch=2, grid=(d // tn, m // tm),
            in_specs=in_specs,
            out_specs=pl.BlockSpec((tm, tn), lambda j, i, te, tv: (i, j))),
        out_shape=jax.ShapeDtypeStruct((m, d), out_dtype),
        compiler_params=_cparams("parallel", "arbitrary"),
        name="ffn_down",
    )(tile_expert, tile_valid, *args)


def _route_kernel(lg_ref, info_ref, cnt_ref, run):
    @pl.when(pl.program_id(0) == 0)
    def _():
        run[...] = jnp.zeros_like(run)

    tb = lg_ref.shape[0]
    lane = lax.broadcasted_iota(jnp.int32, (tb, LANES), 1)
    lanef = lane.astype(F32)
    valid = lane < N_EXPERTS
    lg = jnp.where(valid, lg_ref[...], -jnp.inf)
    ex = jnp.exp(lg - jnp.max(lg, axis=1, keepdims=True))
    probs = ex / jnp.sum(ex, axis=1, keepdims=True)
    p1 = jnp.max(probs, axis=1, keepdims=True)
    e1 = jnp.min(jnp.where(probs == p1, lanef, float(LANES)), axis=1, keepdims=True)
    rest = jnp.where((lanef == e1) | ~valid, -1.0, probs)
    p2 = jnp.max(rest, axis=1, keepdims=True)
    e2 = jnp.min(jnp.where(rest == p2, lanef, float(LANES)), axis=1, keepdims=True)
    w1 = p1 / (p1 + p2)
    w2 = p2 / (p1 + p2)
    oh1 = (lanef == e1).astype(F32)
    oh2 = (lanef == e2).astype(F32)
    both = oh1 + oh2
    before = run[...] + jnp.dot(_tri(tb, strict=True).astype(BF16), both.astype(BF16),
                                preferred_element_type=F32)
    r1 = jnp.sum(before * oh1, axis=1, keepdims=True)
    r2 = jnp.sum(before * oh2, axis=1, keepdims=True)
    info = jnp.where(lane == 0, e1, jnp.where(lane == 1, e2, jnp.where(lane == 2, r1, jnp.where(
        lane == 3, r2, jnp.where(lane == 4, w1, jnp.where(lane == 5, w2, 0.0))))))
    info_ref[...] = info
    run[...] = run[...] + jnp.sum(both, axis=0, keepdims=True)
    cnt_ref[0] = run[...]


def _route(logits):
    t = logits.shape[0]
    tb = min(t, TOK_BLOCK)
    return pl.pallas_call(
        _route_kernel,
        grid=(t // tb,),
        in_specs=[pl.BlockSpec((tb, LANES), lambda i: (i, 0))],
        out_specs=[pl.BlockSpec((tb, LANES), lambda i: (i, 0)),
                   pl.BlockSpec((1, 1, LANES), lambda i: (i, 0, 0))],
        out_shape=[jax.ShapeDtypeStruct((t, LANES), F32),
                   jax.ShapeDtypeStruct((t // tb, 1, LANES), F32)],
        scratch_shapes=[pltpu.VMEM((1, LANES), F32)],
        compiler_params=_cparams("arbitrary"),
        name="route_top2",
    )(logits)


def _dispatch_kernel(act_ref, fetch_ref, h_ref, p1_ref, p2_ref, w1_ref, w2_ref,
                     o_ref, ws_ref, acc, wacc):
    i, kb = pl.program_id(0), pl.program_id(1)
    tm, tb = o_ref.shape[0], h_ref.shape[0]

    @pl.when(kb == 0)
    def _():
        acc[...] = jnp.zeros_like(acc)
        wacc[...] = jnp.zeros_like(wacc)

    @pl.when(act_ref[i * pl.num_programs(1) + kb] == 1)
    def _():
        rowpos = i * tm + lax.broadcasted_iota(jnp.int32, (tm, tb), 0)
        m1 = rowpos == p1_ref[0]
        m2 = rowpos == p2_ref[0]
        sel = jnp.where(m1 | m2, 1.0, 0.0).astype(BF16)
        acc[...] += jnp.dot(sel, h_ref[...], preferred_element_type=F32)
        wacc[...] += jnp.sum(jnp.where(m1, w1_ref[0], 0.0) + jnp.where(m2, w2_ref[0], 0.0),
                             axis=1, keepdims=True)

    @pl.when(kb == pl.num_programs(1) - 1)
    def _():
        o_ref[...] = acc[...].astype(o_ref.dtype)
        ws_ref[...] = wacc[...]


def _dispatch(h, plan):
    t, d = h.shape
    tb = min(t, DISPATCH_TOK)
    nb = t // tb
    tm = ROW_TILE
    n_rows = plan["n_rows"]
    fetch = lambda i, kb, act, fet: fet[i * nb + kb]
    rowv = pl.BlockSpec((1, 1, tb), lambda i, kb, act, fet: (fetch(i, kb, act, fet), 0, 0))
    r3 = lambda x: x.reshape(nb, 1, tb)
    return pl.pallas_call(
        _dispatch_kernel,
        grid_spec=pltpu.PrefetchScalarGridSpec(
            num_scalar_prefetch=2, grid=(n_rows // tm, nb),
            in_specs=[pl.BlockSpec((tb, d), lambda i, kb, act, fet: (fetch(i, kb, act, fet), 0)),
                      rowv, rowv, rowv, rowv],
            out_specs=[pl.BlockSpec((tm, d), lambda i, kb, act, fet: (i, 0)),
                       pl.BlockSpec((tm, 1), lambda i, kb, act, fet: (i, 0))],
            scratch_shapes=[pltpu.VMEM((tm, d), F32), pltpu.VMEM((tm, 1), F32)]),
        out_shape=[jax.ShapeDtypeStruct((n_rows, d), BF16), jax.ShapeDtypeStruct((n_rows, 1), F32)],
        compiler_params=_cparams("parallel", "arbitrary"),
        name="moe_dispatch",
    )(plan["d_act"], plan["d_fetch"], h, r3(plan["pos1"]), r3(plan["pos2"]), r3(plan["w1"]),
      r3(plan["w2"]))


def _combine_kernel(act_ref, fetch_ref, y_ref, p1_ref, p2_ref, x_ref, g_ref, nw_ref, sh_ref, sc_ref,
                    *rest, mode):
    tbk, i = pl.program_id(0), pl.program_id(1)
    acc = rest[-1]
    tb, tm = x_ref.shape[0], y_ref.shape[0]

    @pl.when(i == 0)
    def _():
        acc[...] = jnp.zeros_like(acc)

    @pl.when(act_ref[tbk * pl.num_programs(1) + i] == 1)
    def _():
        rowpos = i * tm + lax.broadcasted_iota(jnp.int32, (tb, tm), 1)
        sel = jnp.where((rowpos == p1_ref[...]) | (rowpos == p2_ref[...]), 1.0, 0.0).astype(BF16)
        acc[...] += jnp.dot(sel, y_ref[...], preferred_element_type=F32)

    @pl.when(i == pl.num_programs(1) - 1)
    def _():
        xn = x_ref[...] + g_ref[...] * acc[...]
        if mode == "final":
            rest[0][...] = _rms(xn, nw_ref[...])
        else:
            rest[0][...] = xn
            rest[1][...] = _norm_mod(xn, nw_ref[...], sh_ref[...], sc_ref[...]).astype(rest[1].dtype)


def _combine(y, plan, x, gate, nw, shift, scale, mode):
    t, d = x.shape
    n_rows = y.shape[0]
    tb = min(t, TOK_BLOCK)
    tm = COMBINE_ROWS
    nr = n_rows // tm
    fetch = lambda tbk, i, act, fet: fet[tbk * nr + i]
    tok = lambda w: pl.BlockSpec((tb, w), lambda tbk, i, act, fet: (tbk, 0))
    vec = pl.BlockSpec((1, d), lambda tbk, i, act, fet: (0, 0))
    if mode == "final":
        out_specs, out_shape = [tok(d)], [jax.ShapeDtypeStruct((t, d), F32)]
    else:
        out_specs = [tok(d), tok(d)]
        out_shape = [jax.ShapeDtypeStruct((t, d), F32), jax.ShapeDtypeStruct((t, d), BF16)]
    return pl.pallas_call(
        functools.partial(_combine_kernel, mode=mode),
        grid_spec=pltpu.PrefetchScalarGridSpec(
            num_scalar_prefetch=2, grid=(t // tb, nr),
            in_specs=[pl.BlockSpec((tm, d), lambda tbk, i, act, fet: (fetch(tbk, i, act, fet), 0)),
                      tok(1), tok(1), tok(d), vec, vec, vec, vec],
            out_specs=out_specs,
            scratch_shapes=[pltpu.VMEM((tb, d), F32)]),
        out_shape=out_shape,
        compiler_params=_cparams("parallel", "arbitrary"),
        name="moe_combine_" + mode,
    )(plan["c_act"], plan["c_fetch"], y, plan["pos1"].reshape(t, 1), plan["pos2"].reshape(t, 1),
      x, gate, nw, shift, scale)


def _skip_tables(active):
    n_in = active.shape[1]
    idx = jnp.where(active, jnp.arange(n_in, dtype=jnp.int32)[None, :], -1)
    run = lax.cummax(idx, axis=1)
    first = jnp.argmax(active, axis=1).astype(jnp.int32)[:, None]
    fetch = jnp.where(run >= 0, run, first)
    return active.astype(jnp.int32).reshape(-1), fetch.astype(jnp.int32).reshape(-1)


def _moe_plan(info, counts, t):
    tm = ROW_TILE
    n_tiles = 2 * t // tm + N_EXPERTS
    n_rows = n_tiles * tm
    e1, e2 = info[:, 0].astype(jnp.int32), info[:, 1].astype(jnp.int32)
    r1, r2 = info[:, 2].astype(jnp.int32), info[:, 3].astype(jnp.int32)
    after = counts[:, 0, :N_EXPERTS].astype(jnp.int32)
    cum = jnp.concatenate([jnp.zeros((1, N_EXPERTS), jnp.int32), after], axis=0)
    total = after[-1]
    padded = (total + tm - 1) // tm * tm
    ends = jnp.cumsum(padded)
    starts = ends - padded
    tile_start = jnp.arange(n_tiles, dtype=jnp.int32) * tm
    tile_expert = jnp.minimum(jnp.sum((tile_start[:, None] >= ends[None, :]).astype(jnp.int32), axis=1),
                              N_EXPERTS - 1).astype(jnp.int32)
    tile_valid = (tile_start < ends[-1]).astype(jnp.int32)

    def overlap(tok_block, row_block):
        c = cum[::tok_block // min(t, TOK_BLOCK)]
        lo = (starts[None, :] + c[:-1])[:, None, :]
        hi = (starts[None, :] + c[1:])[:, None, :]
        r0 = (jnp.arange(n_rows // row_block, dtype=jnp.int32) * row_block)[None, :, None]
        return jnp.any((hi > lo) & (lo < r0 + row_block) & (hi > r0), axis=2)

    d_act, d_fetch = _skip_tables(overlap(min(t, DISPATCH_TOK), tm).T)
    c_act, c_fetch = _skip_tables(overlap(min(t, TOK_BLOCK), COMBINE_ROWS))
    return dict(pos1=starts[e1] + r1, pos2=starts[e2] + r2, w1=info[:, 4], w2=info[:, 5],
                tile_expert=tile_expert, tile_valid=tile_valid, n_rows=n_rows,
                d_act=d_act, d_fetch=d_fetch, c_act=c_act, c_fetch=c_fetch)


def _permute_w_in(w):
    splits = (3072, 8, 8, 1024, 1024, 1536, 16, 1024, 1024, 1024, 1024, 512, 512, 1024, 16, 1024,
              N_BRANCH * D_MODEL)
    off = np.concatenate([[0], np.cumsum(splits)])
    seg = lambda i: w[:, off[i]:off[i + 1]]
    (qkv, gb, ga, ggate, sz, sxbc, sdt, rq, rk, rv, rg, lq, lk, lv, lr, lgate, mrg) = (
        seg(i) for i in range(len(splits)))
    small_pad = jnp.zeros((w.shape[0], NP - OFF_SMALL - 48), w.dtype)
    return jnp.concatenate([qkv, ggate, sz, sxbc, rq, rk, rv, rg, lq, lk, lv, lgate, mrg,
                            gb, ga, sdt, lr, small_pad], axis=1).astype(BF16)


def kernel(x, c, positions, ada_w, ada_b, norm_mix, norm_ffn, w_in, gdn_conv_w, gdn_a_log, gdn_dt_bias, gdn_norm_w, ssd_conv_w, ssd_conv_b, ssd_a_log, ssd_dt_bias, ssd_d, ssd_norm_w, gla_gate_w2, gla_gate_b2, gla_norm_w, w_branch, w_out, ffn_w_gate, ffn_w_up, ffn_w_down, moe_router, moe_w_gate, moe_w_up, moe_w_down, norm_final):
    bsz, t, d = x.shape
    assert bsz == 1 and d == D_MODEL and t % 256 == 0
    xt = x.reshape(t, d)
    depth = ada_w.shape[0]
    tb = min(t, 256)

    mod = _ada_mod(c, ada_w, ada_b)
    cosf, sinf = _rope_tables(positions)
    vec = lambda v: v.reshape(1, d)
    ones_tile = jnp.ones((t // ROW_TILE,), jnp.int32)
    zeros_tile = jnp.zeros((t // ROW_TILE,), jnp.int32)

    h = None
    for i in range(depth):
        shift_m, scale_m, gate_m, shift_f, scale_f, gate_f = (
            mod[i, :, j * d:(j + 1) * d] for j in range(6))
        if i == 0:
            h = _prenorm(xt, vec(norm_mix[0]), shift_m, scale_m)
        p = _matmul(h, _permute_w_in(w_in[i]), F32, 1024, 1024, "in_proj")
        outs = (_gdn_branch(p, gdn_conv_w[i], gdn_a_log[i], gdn_dt_bias[i], gdn_norm_w[i], tb),
                _ssd_branch(p, ssd_conv_w[i], ssd_conv_b[i], ssd_a_log[i], ssd_dt_bias[i], ssd_d[i],
                            ssd_norm_w[i], tb),
                _ret_branch(p, cosf, sinf, tb),
                _gla_branch(p, gla_gate_w2[i], gla_gate_b2[i], gla_norm_w[i], tb))
        merged = _merge(outs, p, w_branch[i].astype(BF16))
        y = _matmul(merged, w_out[i].astype(BF16), F32, 1024, 1024, "out_proj")
        j = i // 2
        last = i == depth - 1
        nxt_nw = norm_final if last else norm_mix[i + 1]
        nxt = (None, None) if last else (mod[i + 1, :, 0:d], mod[i + 1, :, d:2 * d])
        if i % 2 == 0:
            xt, h = _residual(xt, y, gate_m, vec(norm_ffn[i]), shift_f, scale_f)
            act = _ffn_a(h, ffn_w_gate[j][None].astype(BF16), ffn_w_up[j][None].astype(BF16),
                         zeros_tile, ones_tile, ROW_TILE)
            f = _ffn_b(act, ffn_w_down[j][None].astype(BF16), zeros_tile, ones_tile, ROW_TILE)
            if last:
                return _residual(xt, f, gate_f, vec(nxt_nw), gate_f, gate_f, mode="final").reshape(x.shape)
            xt, h = _residual(xt, f, gate_f, vec(nxt_nw), nxt[0], nxt[1])
        else:
            w_r = jnp.zeros((d, LANES), F32).at[:, :N_EXPERTS].set(moe_router[j])
            xt, h, logits = _residual(xt, y, gate_m, vec(norm_ffn[i]), shift_f, scale_f,
                                      mode="router", w_router=w_r)
            info, counts = _route(logits)
            plan = _moe_plan(info, counts, t)
            hs, ws = _dispatch(h, plan)
            act = _ffn_a(hs, moe_w_gate[j].astype(BF16), moe_w_up[j].astype(BF16),
                         plan["tile_expert"], plan["tile_valid"], ROW_TILE)
            yw = _ffn_b(act, moe_w_down[j].astype(BF16), plan["tile_expert"], plan["tile_valid"],
                        ROW_TILE, row_scale=ws, out_dtype=BF16)
            if last:
                (out,) = _combine(yw, plan, xt, gate_f, vec(nxt_nw), gate_f, gate_f, "final")
                return out.reshape(x.shape)
            xt, h = _combine(yw, plan, xt, gate_f, vec(nxt_nw), nxt[0], nxt[1], "plain")
    return None
```
